```python
import math
import jax, jax.numpy as jnp
from jax import lax
import numpy as np

D_MODEL = 1024
BATCH = 2
SEQ = 8192
DEPTH = 2
DEC_BATCH = 128
DEC_SEQ = 4
PAST_LEN = 2048
PAGE_SIZE = 128

D_MIX = D_MODEL
HG_WIDTH = D_MIX // 2
HG_KDIM = 128
HG_HEADS = HG_WIDTH // HG_KDIM
HG_VDIM = HG_WIDTH // HG_HEADS
FOX_WIDTH = D_MIX - HG_WIDTH
FOX_HEAD_DIM = 64
FOX_HEADS = FOX_WIDTH // FOX_HEAD_DIM
HG_CHUNK = 64
Q_BLOCK = 128
NORM_EPS = 1e-6
IN_COLS = 4 * HG_WIDTH + 4 * FOX_WIDTH + FOX_HEADS

kernel_name = "hymba_hgrn2_fox_decoder_step"


def rmsnorm(x, w):
    xf = x.astype(jnp.float32)
    y = xf * lax.rsqrt(jnp.mean(xf * xf, axis=-1, keepdims=True) + NORM_EPS)
    return (y * w.astype(jnp.float32)).astype(x.dtype)


def mixer_inputs(h, w_in_l, b_f_l, lb_l):
    B, T = h.shape[0], h.shape[1]
    f32 = jnp.float32
    proj = jnp.einsum('btd,dc->btc', h, w_in_l)
    sizes = [HG_WIDTH] * 4 + [FOX_WIDTH] * 3 + [FOX_HEADS, FOX_WIDTH]
    offs = [int(o) for o in np.cumsum(sizes)[:-1]]
    hq, hf, hi, hgate, fq, fk, fv, ff, fgate = jnp.split(proj, offs, axis=-1)
    z = hf.reshape(B, T, HG_HEADS, HG_KDIM).astype(f32)
    lb = lb_l.reshape(HG_HEADS, HG_KDIM).astype(f32)
    hg_logf = jnp.logaddexp(jnp.log(lb), jnp.log1p(-lb) + jax.nn.log_sigmoid(z))
    hg_k = (1.0 - lb) * jax.nn.sigmoid(-z)
    hg_q = hq.reshape(B, T, HG_HEADS, HG_KDIM).astype(f32) * (HG_KDIM ** -0.5)
    hg_v = hi.reshape(B, T, HG_HEADS, HG_VDIM).astype(f32)
    fox_logf = jax.nn.log_sigmoid(ff.astype(f32) + b_f_l.astype(f32))
    fox_q = fq.reshape(B, T, FOX_HEADS, FOX_HEAD_DIM)
    fox_k = fk.reshape(B, T, FOX_HEADS, FOX_HEAD_DIM)
    fox_v = fv.reshape(B, T, FOX_HEADS, FOX_HEAD_DIM)
    return hg_q, hg_k, hg_logf, hg_v, hgate, fox_q, fox_k, fox_v, fox_logf, fgate


def hgrn_chunked(q, k, log_f, v, s0):
    B, T, H, K = q.shape
    V = v.shape[-1]
    C = min(HG_CHUNK, T)
    n = T // C

    def to_chunks(a):
        return jnp.moveaxis(a.reshape(B, n, C, *a.shape[2:]), 1, 0)

    causal = jnp.tril(jnp.ones((C, C), dtype=bool))[None, :, :, None, None]

    def step(S, inp):
        qc, kc, gc, vc = inp
        b = jnp.cumsum(gc, axis=1)
        diff = b[:, :, None] - b[:, None, :]
        decay = jnp.exp(jnp.where(causal, diff, -jnp.inf))
        A = jnp.einsum('bthk,bshk,btshk->bhts', qc, kc, decay)
        o_intra = jnp.einsum('bhts,bshv->bthv', A, vc)
        o_inter = jnp.einsum('bthk,bhkv->bthv', qc * jnp.exp(b), S)
        b_last = b[:, -1]
        k_dec = kc * jnp.exp(b_last[:, None] - b)
        S_new = jnp.exp(b_last)[..., None] * S + jnp.einsum('bshk,bshv->bhkv', k_dec, vc)
        return S_new, o_intra + o_inter

    S_fin, o = lax.scan(step, s0.astype(jnp.float32),
                        (to_chunks(q), to_chunks(k), to_chunks(log_f), to_chunks(v)))
    o = jnp.moveaxis(o, 0, 1).reshape(B, T, H, V)
    return o, S_fin


def fox_prompt(q, k, v, log_f):
    B, T, H, D = q.shape
    scale = D ** -0.5
    c = jnp.cumsum(log_f, axis=1).transpose(0, 2, 1)
    kpos = jnp.arange(T)
    nb = T // Q_BLOCK

    def block(i):
        start = i * Q_BLOCK
        qb = lax.dynamic_slice_in_dim(q, start, Q_BLOCK, axis=1)
        cb = lax.dynamic_slice_in_dim(c, start, Q_BLOCK, axis=2)
        s = jnp.einsum('bqhd,bkhd->bhqk', qb, k).astype(jnp.float32) * scale
        s = s + (cb[..., None] - c[:, :, None, :])
        qpos = start + jnp.arange(Q_BLOCK)
        s = jnp.where(qpos[:, None] >= kpos[None, :], s, -jnp.inf)
        p = jax.nn.softmax(s, axis=-1)
        return jnp.einsum('bhqk,bkhd->bqhd', p.astype(v.dtype), v)

    o = lax.map(block, jnp.arange(nb))
    return jnp.moveaxis(o, 0, 1).reshape(B, T, H, D)


def fox_sample(q, k_new, v_new, logf_new, k_past, v_past, logf_past):
    P = k_past.shape[1]
    Tn = q.shape[1]
    D = q.shape[-1]
    k = jnp.concatenate([k_past.astype(k_new.dtype), k_new], axis=1)
    v = jnp.concatenate([v_past.astype(v_new.dtype), v_new], axis=1)
    lf = jnp.concatenate([logf_past.astype(jnp.float32), logf_new], axis=1)
    c = jnp.cumsum(lf, axis=1).transpose(0, 2, 1)
    cq = c[:, :, P:]
    s = jnp.einsum('bqhd,bkhd->bhqk', q, k).astype(jnp.float32) * (D ** -0.5)
    s = s + (cq[..., None] - c[:, :, None, :])
    qpos = P + jnp.arange(Tn)
    kpos = jnp.arange(P + Tn)
    s = jnp.where(qpos[:, None] >= kpos[None, :], s, -jnp.inf)
    p = jax.nn.softmax(s, axis=-1)
    return jnp.einsum('bhqk,bkhd->bqhd', p.astype(v.dtype), v)


def mixer_output(hg_o, hgate, fox_o, fgate, hg_norm_w_l, w_out_l, dtype):
    B, T = hg_o.shape[0], hg_o.shape[1]
    hg = rmsnorm(hg_o, hg_norm_w_l).reshape(B, T, HG_WIDTH)
    hg = hg * jax.nn.silu(hgate.astype(jnp.float32))
    fx = fox_o.reshape(B, T, FOX_WIDTH).astype(jnp.float32) * jax.nn.silu(fgate.astype(jnp.float32))
    merged = jnp.concatenate([hg, fx], axis=-1).astype(dtype)
    return jnp.einsum('btc,cd->btd', merged, w_out_l)


def setup_inputs(seed: int = 0) -> dict:
    key = jax.random.key(seed)
    ks = jax.random.split(key, 16)
    n_pages = PAST_LEN // PAGE_SIZE
    n_used = DEC_BATCH * n_pages
    n_pool = n_used + n_used // 4
    f32 = jnp.float32
    x_prompt = jax.random.normal(ks[0], (BATCH, SEQ, D_MODEL), f32)
    x_sample = jax.random.normal(ks[1], (DEC_BATCH, DEC_SEQ, D_MODEL), f32)
    cache_k = jax.random.normal(ks[2], (DEPTH, n_pool, PAGE_SIZE, FOX_HEADS, FOX_HEAD_DIM), f32)
    cache_v = jax.random.normal(ks[3], (DEPTH, n_pool, PAGE_SIZE, FOX_HEADS, FOX_HEAD_DIM), f32)
    cache_logf = jax.nn.log_sigmoid(2.0 + jax.random.normal(ks[4], (DEPTH, n_pool, PAGE_SIZE, FOX_HEADS), f32))
    state_hgrn = 0.3 * jax.random.normal(ks[5], (DEPTH, DEC_BATCH, HG_HEADS, HG_KDIM, HG_VDIM), f32)
    page_table = jax.random.permutation(ks[6], n_pool)[:n_used].reshape(DEC_BATCH, n_pages).astype(jnp.int32)
    norm_w = 1.0 + 0.05 * jax.random.normal(ks[7], (DEPTH, D_MODEL), f32)
    w_in = jax.random.normal(ks[8], (DEPTH, D_MODEL, IN_COLS), f32) * D_MODEL ** -0.5
    b_fox_f = 2.0 + 0.5 * jax.random.normal(ks[9], (DEPTH, FOX_HEADS), f32)
    hg_lb = 0.1 * jax.random.normal(ks[10], (DEPTH, HG_WIDTH), f32)
    hg_norm_w = 1.0 + 0.05 * jax.random.normal(ks[11], (DEPTH, HG_VDIM), f32)
    w_out = jax.random.normal(ks[12], (DEPTH, D_MIX, D_MODEL), f32) * D_MIX ** -0.5
    final_norm_w = 1.0 + 0.05 * jax.random.normal(ks[13], (D_MODEL,), f32)
    return {"x_prompt": x_prompt, "x_sample": x_sample, "cache_k": cache_k, "cache_v": cache_v,
            "cache_logf": cache_logf, "state_hgrn": state_hgrn, "page_table": page_table,
            "norm_w": norm_w, "w_in": w_in, "b_fox_f": b_fox_f, "hg_lb": hg_lb,
            "hg_norm_w": hg_norm_w, "w_out": w_out, "final_norm_w": final_norm_w}


def reference(x_prompt, x_sample, cache_k, cache_v, cache_logf, state_hgrn, page_table,
              norm_w, w_in, b_fox_f, hg_lb, hg_norm_w, w_out, final_norm_w):
    n_pages = page_table.shape[1]
    past = n_pages * PAGE_SIZE
    dbat = x_sample.shape[0]
    soft = jax.nn.softmax(hg_lb.astype(jnp.float32), axis=0)
    cum = jnp.cumsum(soft, axis=0)
    lower_bounds = cum - cum[0:1]

    xp, xs = x_prompt, x_sample
    pk_l, pv_l, plf_l, ps_l = [], [], [], []
    sk_l, sv_l, slf_l, ss_l = [], [], [], []
    for l in range(DEPTH):
        hp = rmsnorm(xp, norm_w[l])
        hq, hk, hlf, hv, hgate, fq, fk, fv, flf, fgate = mixer_inputs(hp, w_in[l], b_fox_f[l], lower_bounds[l])
        s0 = jnp.zeros((xp.shape[0], HG_HEADS, HG_KDIM, HG_VDIM), jnp.float32)
        hg_o, S_p = hgrn_chunked(hq, hk, hlf, hv, s0)
        fox_o = fox_prompt(fq, fk, fv, flf)
        xp = xp + mixer_output(hg_o, hgate, fox_o, fgate, hg_norm_w[l], w_out[l], xp.dtype)
        pk_l.append(fk); pv_l.append(fv); plf_l.append(flf); ps_l.append(S_p)
        hs = rmsnorm(xs, norm_w[l])
        sq, sk, slf, sv, sgate, gq, gk, gv, glf, ggate = mixer_inputs(hs, w_in[l], b_fox_f[l], lower_bounds[l])
        hg_os, S_s = hgrn_chunked(sq, sk, slf, sv, state_hgrn[l])
        k_past = cache_k[l, page_table].reshape(dbat, past, FOX_HEADS, FOX_HEAD_DIM)
        v_past = cache_v[l, page_table].reshape(dbat, past, FOX_HEADS, FOX_HEAD_DIM)
        lf_past = cache_logf[l, page_table].reshape(dbat, past, FOX_HEADS)
        fox_os = fox_sample(gq, gk, gv, glf, k_past, v_past, lf_past)
        xs = xs + mixer_output(hg_os, sgate, fox_os, ggate, hg_norm_w[l], w_out[l], xs.dtype)
        sk_l.append(gk); sv_l.append(gv); slf_l.append(glf); ss_l.append(S_s)

    y_prompt = rmsnorm(xp, final_norm_w)
    y_sample = rmsnorm(xs, final_norm_w)
    prompt_k = jnp.stack(pk_l); prompt_v = jnp.stack(pv_l)
    prompt_logf = jnp.stack(plf_l); prompt_state_hgrn = jnp.stack(ps_l)
    sample_k = jnp.stack(sk_l); sample_v = jnp.stack(sv_l)
    sample_logf = jnp.stack(slf_l); sample_state_hgrn = jnp.stack(ss_l)
    return (y_prompt, y_sample, prompt_k, prompt_v, prompt_logf, prompt_state_hgrn,
            sample_k, sample_v, sample_logf, sample_state_hgrn)
```

```python
import functools

import numpy as np
import jax
import jax.numpy as jnp
from jax import lax
from jax.experimental import pallas as pl
from jax.experimental.pallas import tpu as pltpu

F32 = jnp.float32
BF16 = jnp.bfloat16

LANES = 128
SUBLANES = 8
VMEM_LIMIT_BYTES = 56 * 1024 * 1024

HG_HEADS = 4
HG_KDIM = 128
HG_WIDTH = HG_HEADS * HG_KDIM
FOX_HEADS = 8
FOX_HEAD_DIM = 64
FOX_WIDTH = FOX_HEADS * FOX_HEAD_DIM
FOX_AUG = FOX_HEADS * LANES
NORM_EPS = 1e-6
N_SPLIT = 3

TOK_TILE = 256
HG_CHUNK = 128
HG_BASE = SUBLANES
ATT_BLOCK = 512
SAMPLE_ROWS = SUBLANES
HG_SAMPLE_SEQS = 8


def _cparams(n_grid):
    return pltpu.CompilerParams(dimension_semantics=("arbitrary",) * n_grid,
                                vmem_limit_bytes=VMEM_LIMIT_BYTES)


def _split3(x):
    p0 = x.astype(BF16)
    r = x - p0.astype(F32)
    p1 = r.astype(BF16)
    p2 = (r - p1.astype(F32)).astype(BF16)
    return p0, p1, p2


def _dot(a, b):
    return jnp.dot(a, b, preferred_element_type=F32)


def _dot_nt(a, b):
    return lax.dot_general(a, b, (((1,), (1,)), ((), ())), preferred_element_type=F32)


def _dot_tn(a, b):
    return lax.dot_general(a, b, (((0,), (0,)), ((), ())), preferred_element_type=F32)


def _dot_exact_lhs(mat_bf16, x):
    p0, p1, p2 = _split3(x)
    return _dot(mat_bf16, p0) + _dot(mat_bf16, p1) + _dot(mat_bf16, p2)


def _log_sigmoid(z):
    return jnp.minimum(z, 0.0) - jnp.log(1.0 + jnp.exp(-jnp.abs(z)))


def _silu(g):
    return g / (1.0 + jnp.exp(-g))


def _in_proj_body(layer, tiles_per_seq,
                  x_ref, nw_ref, wm_ref, wff_ref, bf_ref, lb_ref, tri_ref, place_ref, cst_ref,
                  hq_ref, hk_ref, hlf_ref, hv_ref, hgs_ref,
                  qa_ref, ka_ref, va_ref, fq_ref, fk_ref, fv_ref, flf_ref, fgs_ref, cum_ref,
                  carry_ref):
    i = pl.program_id(0)
    x = x_ref[...]
    ms = jnp.mean(x * x, axis=-1, keepdims=True)
    h = (x * lax.rsqrt(ms + NORM_EPS) * nw_ref[...]).astype(BF16)

    def proj(g):
        return _dot(h, wm_ref[:, g * HG_WIDTH:(g + 1) * HG_WIDTH])

    lbw = lb_ref[...]
    e = jnp.exp(lbw - jnp.max(lbw, axis=0, keepdims=True))
    soft = e / jnp.sum(e, axis=0, keepdims=True)
    lb = jnp.zeros((1, HG_WIDTH), F32)
    for j in range(1, layer + 1):
        lb = lb + soft[j:j + 1, :]

    hq_ref[...] = proj(0) * (HG_KDIM ** -0.5)
    z = proj(1)
    ez = jnp.exp(-jnp.abs(z))
    log_sig = jnp.minimum(z, 0.0) - jnp.log(1.0 + ez)
    a = jnp.log(lb)
    b2 = jnp.log(1.0 - lb) + log_sig
    hlf_ref[...] = jnp.maximum(a, b2) + jnp.log(1.0 + jnp.exp(-jnp.abs(a - b2)))
    hk_ref[...] = (1.0 - lb) * (jnp.where(z >= 0.0, ez, 1.0) / (1.0 + ez))
    hv_ref[...] = proj(2)
    hgs_ref[...] = _silu(proj(3))

    fq = proj(4) * (FOX_HEAD_DIM ** -0.5)
    fk = proj(5)
    fv = proj(6)
    fq_ref[...] = fq
    fk_ref[...] = fk
    fv_ref[...] = fv
    fgs_ref[...] = _silu(proj(7))

    flf = _log_sigmoid(_dot(h, wff_ref[...]) + bf_ref[...])
    flf_ref[...] = flf
    cum = _dot_exact_lhs(tri_ref[...], flf)
    if tiles_per_seq is not None:
        @pl.when(i % tiles_per_seq == 0)
        def _():
            carry_ref[...] = jnp.zeros_like(carry_ref)
        cum = cum + carry_ref[0:1, :]
        carry_ref[0:1, :] = cum[TOK_TILE - 1:TOK_TILE, :]
    cum_ref[...] = cum

    lane = lax.broadcasted_iota(jnp.int32, (TOK_TILE, LANES), 1)
    neg = jnp.where(lane < FOX_HEADS, -cum, 0.0)
    n0, n1, n2 = _split3(neg)
    pieces = (n0.astype(F32) + pltpu.roll(n1.astype(F32), FOX_HEADS, 1)
              + pltpu.roll(n2.astype(F32), 2 * FOX_HEADS, 1)).astype(BF16)
    bias_cols = _dot(pieces, place_ref[...])

    def rep(t):
        return jnp.concatenate(
            [t[:, LANES * (hd // 2):LANES * (hd // 2 + 1)] for hd in range(FOX_HEADS)], axis=1)

    is_data = jnp.broadcast_to(cst_ref[0:1, :], (TOK_TILE, FOX_AUG)) > 0.0
    qa_ref[...] = jnp.where(is_data, rep(fq), cst_ref[1:2, :]).astype(BF16)
    ka_ref[...] = jnp.where(is_data, rep(fk), bias_cols).astype(BF16)
    va_ref[...] = jnp.where(is_data, rep(fv), cst_ref[2:3, :]).astype(BF16)


def _aug_constants():
    cst = np.zeros((SUBLANES, FOX_AUG), np.float32)
    place = np.zeros((LANES, FOX_AUG), np.float32)
    for hd in range(FOX_HEADS):
        base = LANES * hd
        data0 = 0 if hd % 2 == 0 else FOX_HEAD_DIM
        help0 = FOX_HEAD_DIM if hd % 2 == 0 else 0
        cst[0, base + data0:base + data0 + FOX_HEAD_DIM] = 1.0
        for part in range(N_SPLIT):
            cst[1, base + help0 + part] = 1.0
            place[part * FOX_HEADS + hd, base + help0 + part] = 1.0
        cst[2, base + help0] = 1.0
    return jnp.asarray(cst), jnp.asarray(place, dtype=BF16)


def _in_proj(x, layer, norm_w_l, wm, wff, bf, hg_lb, tri, tiles_per_seq):
    n = x.shape[0]
    d_model = x.shape[1]
    assert n % TOK_TILE == 0
    cst, place = _aug_constants()
    depth = hg_lb.shape[0]

    def row(width):
        return pl.BlockSpec((TOK_TILE, width), lambda i: (i, 0))

    def whole(shape):
        return pl.BlockSpec(shape, lambda i: (0,) * len(shape))

    widths_f32 = [HG_WIDTH] * 5
    out_shape = ([jax.ShapeDtypeStruct((n, HG_WIDTH), F32)] * 5
                 + [jax.ShapeDtypeStruct((n, FOX_AUG), BF16)] * 3
                 + [jax.ShapeDtypeStruct((n, FOX_WIDTH), F32)] * 3
                 + [jax.ShapeDtypeStruct((n, LANES), F32),
                    jax.ShapeDtypeStruct((n, FOX_WIDTH), F32),
                    jax.ShapeDtypeStruct((n, LANES), F32)])
    out_specs = ([row(w) for w in widths_f32] + [row(FOX_AUG)] * 3 + [row(FOX_WIDTH)] * 3
                 + [row(LANES), row(FOX_WIDTH), row(LANES)])
    return pl.pallas_call(
        functools.partial(_in_proj_body, layer, tiles_per_seq),
        grid=(n // TOK_TILE,),
        in_specs=[row(d_model), whole((1, d_model)), whole(wm.shape), whole(wff.shape),
                  whole((1, LANES)), whole((depth, HG_WIDTH)), whole((TOK_TILE, TOK_TILE)),
                  whole(place.shape), whole(cst.shape)],
        out_specs=out_specs,
        out_shape=out_shape,
        scratch_shapes=[pltpu.VMEM((SUBLANES, LANES), F32)],
        compiler_params=_cparams(1),
        name=f"in_proj_l{layer}_{'prompt' if tiles_per_seq else 'sample'}",
    )(x, norm_w_l, wm, wff, bf, hg_lb, tri, place, cst)


def _hgrn_prompt_body(q_ref, k_ref, lf_ref, v_ref, gs_ref, nw_ref, tri_ref,
                      o_ref, sfin_ref, st_ref):
    c = HG_CHUNK
    j = pl.program_id(1)

    @pl.when(j == 0)
    def _():
        st_ref[...] = jnp.zeros_like(st_ref)

    tri = tri_ref[...]
    row = lax.broadcasted_iota(jnp.int32, (c, c), 0)
    col = lax.broadcasted_iota(jnp.int32, (c, c), 1)
    rowv = lax.broadcasted_iota(jnp.int32, (c, LANES), 0)
    sub = lax.broadcasted_iota(jnp.int32, (c // HG_BASE, HG_BASE, 1), 1)

    for hd in range(HG_HEADS):
        sl = slice(hd * HG_KDIM, (hd + 1) * HG_KDIM)
        q = q_ref[:, sl]
        k = k_ref[:, sl]
        v = v_ref[:, sl]
        b = _dot_exact_lhs(tri, lf_ref[:, sl])

        nb = c // HG_BASE
        b3 = b.reshape(nb, HG_BASE, HG_KDIM)
        q3 = q.reshape(nb, HG_BASE, HG_KDIM)
        k3 = k.reshape(nb, HG_BASE, HG_KDIM)
        v3 = v.reshape(nb, HG_BASE, HG_KDIM)
        o3 = jnp.zeros((nb, HG_BASE, HG_KDIM), F32)
        for s in range(HG_BASE):
            dec = jnp.exp(jnp.minimum(b3 - b3[:, s:s + 1, :], 0.0))
            w = jnp.sum(q3 * k3[:, s:s + 1, :] * dec, axis=-1, keepdims=True)
            w = jnp.where(sub >= s, w, 0.0)
            o3 = o3 + w * v3[:, s:s + 1, :]
        o = o3.reshape(c, HG_KDIM)

        a_off = jnp.zeros((c, c), F32)
        m = HG_BASE
        while m < c:
            pair = 2 * m
            bp = b.reshape(c // pair, pair, HG_KDIM)
            fac = jnp.exp(-jnp.abs(bp - bp[:, m - 1:m, :])).reshape(c, HG_KDIM)
            is_query = ((rowv >> (m.bit_length() - 1)) & 1) == 1
            qt = jnp.where(is_query, q * fac, 0.0).astype(BF16)
            kt = jnp.where(is_query, 0.0, k * fac).astype(BF16)
            same_pair = (row >> (pair.bit_length() - 1)) == (col >> (pair.bit_length() - 1))
            a_off = a_off + jnp.where(same_pair, _dot_nt(qt, kt), 0.0)
            m = pair
        o = o + _dot(a_off.astype(BF16), v.astype(BF16))

        st = st_ref[hd]
        o = o + _dot_nt((q * jnp.exp(b)).astype(BF16), st.astype(BF16))
        b_last = b[c - 1:c, :]
        k_dec = (k * jnp.exp(b_last - b)).astype(BF16)
        st_ref[hd] = st * jnp.exp(b_last) + _dot_tn(v.astype(BF16), k_dec)

        y = o * lax.rsqrt(jnp.mean(o * o, axis=-1, keepdims=True) + NORM_EPS) * nw_ref[...]
        o_ref[:, sl] = (y * gs_ref[:, sl]).astype(BF16)

    @pl.when(j == pl.num_programs(1) - 1)
    def _():
        for hd in range(HG_HEADS):
            sfin_ref[0, hd] = st_ref[hd].T


def _hgrn_prompt(hq, hk, hlf, hv, hgs, nw, batch, seq, layer):
    c = HG_CHUNK
    assert seq % c == 0
    nc = seq // c
    tri = jnp.asarray(np.tril(np.ones((c, c), np.float32)), dtype=BF16)

    def row():
        return pl.BlockSpec((c, HG_WIDTH), lambda b, j: (b * nc + j, 0))

    return pl.pallas_call(
        _hgrn_prompt_body,
        grid=(batch, nc),
        in_specs=[row(), row(), row(), row(), row(),
                  pl.BlockSpec((1, HG_KDIM), lambda b, j: (0, 0)),
                  pl.BlockSpec((c, c), lambda b, j: (0, 0))],
        out_specs=[row(),
                   pl.BlockSpec((1, HG_HEADS, HG_KDIM, HG_KDIM), lambda b, j: (b, 0, 0, 0))],
        out_shape=[jax.ShapeDtypeStruct((batch * seq, HG_WIDTH), BF16),
                   jax.ShapeDtypeStruct((batch, HG_HEADS, HG_KDIM, HG_KDIM), F32)],
        scratch_shapes=[pltpu.VMEM((HG_HEADS, HG_KDIM, HG_KDIM), F32)],
        compiler_params=_cparams(2),
        name=f"hgrn_prompt_l{layer}",
    )(hq, hk, hlf, hv, hgs, nw, tri)


def _fox_prompt_body(q_ref, k_ref, v_ref, g_ref, o_ref):
    t = ATT_BLOCK
    i = pl.program_id(2)
    row = lax.broadcasted_iota(jnp.int32, (t, t), 0)
    col = lax.broadcasted_iota(jnp.int32, (t, t), 1)
    lane = lax.broadcasted_iota(jnp.int32, (t, LANES), 1)
    heads = []
    for hh in range(2):
        sl = slice(hh * LANES, (hh + 1) * LANES)
        q = q_ref[:, sl]

        def step(j, carry, diagonal):
            m, acc = carry
            start = pl.multiple_of(j * t, t)
            s = _dot_nt(q, k_ref[pl.ds(start, t), sl])
            if diagonal:
                s = jnp.where(col <= row, s, -jnp.inf)
            m_new = jnp.maximum(m, jnp.max(s, axis=-1, keepdims=True))
            p = jnp.exp(s - m_new)
            acc = jnp.exp(m - m_new) * acc + _dot(p.astype(BF16), v_ref[pl.ds(start, t), sl])
            return m_new, acc

        carry = (jnp.full((t, 1), -jnp.inf, F32), jnp.zeros((t, LANES), F32))
        carry = lax.fori_loop(0, i, functools.partial(step, diagonal=False), carry)
        _, acc = step(i, carry, True)
        ones_lane = FOX_HEAD_DIM if hh == 0 else 0
        heads.append(acc / acc[:, ones_lane:ones_lane + 1])
    o = jnp.where(lane < FOX_HEAD_DIM, heads[0], heads[1])
    o_ref[...] = (o * g_ref[...]).astype(BF16)


def _fox_prompt(qa, ka, va, fgs, batch, seq, layer):
    t = ATT_BLOCK
    assert seq % t == 0
    nq = seq // t
    return pl.pallas_call(
        _fox_prompt_body,
        grid=(batch, FOX_HEADS // 2, nq),
        in_specs=[pl.BlockSpec((t, 2 * LANES), lambda b, p, i: (b * nq + i, p)),
                  pl.BlockSpec((seq, 2 * LANES), lambda b, p, i: (b, p)),
                  pl.BlockSpec((seq, 2 * LANES), lambda b, p, i: (b, p)),
                  pl.BlockSpec((t, LANES), lambda b, p, i: (b * nq + i, p))],
        out_specs=pl.BlockSpec((t, LANES), lambda b, p, i: (b * nq + i, p)),
        out_shape=jax.ShapeDtypeStruct((batch * seq, FOX_WIDTH), BF16),
        compiler_params=_cparams(3),
        name=f"fox_prompt_l{layer}",
    )(qa, ka, va, fgs)


def _hgrn_sample_body(q_ref, k_ref, lf_ref, v_ref, gs_ref, nw_ref, s_ref, o_ref, sout_ref):
    r = SAMPLE_ROWS
    rowv = lax.broadcasted_iota(jnp.int32, (r, HG_KDIM), 0)
    ones_rows = jnp.where(rowv < N_SPLIT, 1.0, 0.0).astype(BF16)
    for bb in range(HG_SAMPLE_SEQS):
        for hd in range(HG_HEADS):
            sl = slice(hd * HG_KDIM, (hd + 1) * HG_KDIM)
            q = q_ref[bb, :, sl]
            k = k_ref[bb, :, sl]
            v = v_ref[bb, :, sl]
            g = lf_ref[bb, :, sl]
            b = g
            for n in range(1, r):
                b = b + jnp.where(rowv >= n, pltpu.roll(g, n, 0), 0.0)
            o = jnp.zeros((r, HG_KDIM), F32)
            for s in range(r):
                dec = jnp.exp(jnp.minimum(b - b[s:s + 1, :], 0.0))
                w = jnp.sum(q * k[s:s + 1, :] * dec, axis=-1, keepdims=True)
                o = o + jnp.where(rowv[:, 0:1] >= s, w, 0.0) * v[s:s + 1, :]
            st = s_ref[bb, hd]
            o = o + _dot((q * jnp.exp(b)).astype(BF16), st.astype(BF16))
            b_last = b[r - 1:r, :]
            k_dec = (k * jnp.exp(b_last - b)).astype(BF16)
            d0, d1, d2 = (d.astype(F32) for d in _split3(jnp.exp(b_last)))
            dec_rows = jnp.where(rowv == 0, d0, jnp.where(rowv == 1, d1, jnp.where(rowv == 2, d2, 0.0)))
            dec_col = _dot_tn(dec_rows.astype(BF16), ones_rows)
            sout_ref[bb, hd] = st * dec_col + _dot_tn(k_dec, v.astype(BF16))
            y = o * lax.rsqrt(jnp.mean(o * o, axis=-1, keepdims=True) + NORM_EPS) * nw_ref[...]
            o_ref[bb, :, sl] = y * gs_ref[bb, :, sl]


def _hgrn_sample(hq, hk, hlf, hv, hgs, nw, state_l, layer):
    nb = state_l.shape[0]
    g = HG_SAMPLE_SEQS
    assert nb % g == 0

    def row():
        return pl.BlockSpec((g, SAMPLE_ROWS, HG_WIDTH), lambda i: (i, 0, 0))

    def st():
        return pl.BlockSpec((g, HG_HEADS, HG_KDIM, HG_KDIM), lambda i: (i, 0, 0, 0))

    return pl.pallas_call(
        _hgrn_sample_body,
        grid=(nb // g,),
        in_specs=[row(), row(), row(), row(), row(),
                  pl.BlockSpec((1, HG_KDIM), lambda i: (0, 0)), st()],
        out_specs=[row(), st()],
        out_shape=[jax.ShapeDtypeStruct((nb, SAMPLE_ROWS, HG_WIDTH), F32),
                   jax.ShapeDtypeStruct(state_l.shape, F32)],
        compiler_params=_cparams(1),
        name=f"hgrn_sample_l{layer}",
    )(hq, hk, hlf, hv, hgs, nw, state_l)


def _fox_sample_body(n_pages, dec_seq, pt_ref, *refs):
    del pt_ref
    k_refs = refs[:n_pages]
    v_refs = refs[n_pages:2 * n_pages]
    lf_refs = refs[2 * n_pages:3 * n_pages]
    (fq_ref, kn_ref, vn_ref, cn_ref, gs_ref, hm_ref, sel_ref, suf_ref, o_ref) = refs[3 * n_pages:]
    r = SAMPLE_ROWS
    nrow = dec_seq * FOX_HEADS
    hm = hm_ref[...]
    fq = fq_ref[0]
    qp = jnp.concatenate([jnp.broadcast_to(fq[i:i + 1, :], (FOX_HEADS, FOX_WIDTH))
                          for i in range(dec_seq)], axis=0) * hm
    qb = qp.astype(BF16)

    lf_all = jnp.concatenate([lf_refs[j][0, 0] for j in range(n_pages)], axis=0)
    p0, p1, p2 = _split3(lf_all)
    suf = suf_ref[...]
    ones = jnp.ones((PAGE, LANES), BF16)
    within = _dot(p0, suf) + _dot(p1, suf) + _dot(p2, suf)
    total = _dot(p0, ones) + _dot(p1, ones) + _dot(p2, ones)
    later = jnp.zeros((FOX_HEADS, LANES), F32)
    scores = [None] * n_pages
    for j in reversed(range(n_pages)):
        rs = slice(j * FOX_HEADS, (j + 1) * FOX_HEADS)
        bias = within[rs] + later
        later = later + total[rs]
        s = _dot(qb, k_refs[j][0, 0].astype(BF16))
        scores[j] = s + jnp.concatenate([bias] * dec_seq, axis=0)
    s_past = jnp.concatenate(scores, axis=1)

    lane = lax.broadcasted_iota(jnp.int32, (r, LANES), 1)
    cn = jnp.where(lane < FOX_HEADS, cn_ref[0], 0.0)
    c0, c1, c2 = _split3(cn)
    pieces = (c0.astype(F32) + pltpu.roll(c1.astype(F32), FOX_HEADS, 1)
              + pltpu.roll(c2.astype(F32), 2 * FOX_HEADS, 1)).astype(BF16)
    s_new = _dot_nt(qb, kn_ref[0].astype(BF16)) - _dot_nt(sel_ref[...], pieces)
    qi = lax.broadcasted_iota(jnp.int32, (nrow, r), 0) >> (FOX_HEADS.bit_length() - 1)
    kj = lax.broadcasted_iota(jnp.int32, (nrow, r), 1)
    s_new = jnp.where(kj <= qi, s_new, -jnp.inf)

    m = jnp.maximum(jnp.max(s_past, axis=-1, keepdims=True), jnp.max(s_new, axis=-1, keepdims=True))
    p_past = jnp.exp(s_past - m)
    p_new = jnp.exp(s_new - m)
    denom = jnp.sum(p_past, axis=-1, keepdims=True) + jnp.sum(p_new, axis=-1, keepdims=True)
    acc = _dot(p_new.astype(BF16), vn_ref[0].astype(BF16))
    for j in range(n_pages):
        acc = acc + _dot_nt(p_past[:, j * PAGE:(j + 1) * PAGE].astype(BF16),
                            v_refs[j][0, 0].astype(BF16))
    acc = acc / denom * hm
    o_ref[0] = jnp.zeros((r, FOX_WIDTH), F32)
    for i in range(dec_seq):
        own = jnp.sum(acc[i * FOX_HEADS:(i + 1) * FOX_HEADS, :], axis=0, keepdims=True)
        o_ref[0, i:i + 1, :] = own * gs_ref[0, i:i + 1, :]


PAGE = 128


def _fox_sample(layer, page_table, kT, vT, lfT, fq, kn, vn, cn, fgs, dec_seq):
    nb, n_pages = page_table.shape
    nrow = dec_seq * FOX_HEADS
    hm = np.zeros((nrow, FOX_WIDTH), np.float32)
    sel = np.zeros((nrow, LANES), np.float32)
    for i in range(dec_seq):
        for hd in range(FOX_HEADS):
            hm[i * FOX_HEADS + hd, hd * FOX_HEAD_DIM:(hd + 1) * FOX_HEAD_DIM] = 1.0
            for part in range(N_SPLIT):
                sel[i * FOX_HEADS + hd, part * FOX_HEADS + hd] = 1.0
    suf = np.triu(np.ones((PAGE, PAGE), np.float32), 1).T

    def page_spec(j, rows):
        return pl.BlockSpec((1, 1, rows, PAGE), lambda b, pt: (layer, pt[b, j], 0, 0))

    def seq_spec(width):
        return pl.BlockSpec((1, SAMPLE_ROWS, width), lambda b, pt: (b, 0, 0))

    def whole(shape):
        return pl.BlockSpec(shape, lambda b, pt: (0,) * len(shape))

    in_specs = ([page_spec(j, FOX_WIDTH) for j in range(n_pages)]
                + [page_spec(j, FOX_WIDTH) for j in range(n_pages)]
                + [page_spec(j, FOX_HEADS) for j in range(n_pages)]
                + [seq_spec(FOX_WIDTH), seq_spec(FOX_WIDTH), seq_spec(FOX_WIDTH), seq_spec(LANES),
                   seq_spec(FOX_WIDTH), whole(hm.shape), whole(sel.shape), whole(suf.shape)])
    grid_spec = pltpu.PrefetchScalarGridSpec(
        num_scalar_prefetch=1, grid=(nb,), in_specs=in_specs,
        out_specs=seq_spec(FOX_WIDTH))
    return pl.pallas_call(
        functools.partial(_fox_sample_body, n_pages, dec_seq),
        grid_spec=grid_spec,
        out_shape=jax.ShapeDtypeStruct((nb, SAMPLE_ROWS, FOX_WIDTH), F32),
        compiler_params=_cparams(1),
        name=f"fox_sample_l{layer}",
    )(page_table, *([kT] * n_pages), *([vT] * n_pages), *([lfT] * n_pages),
      fq, kn, vn, cn, fgs, jnp.asarray(hm), jnp.asarray(sel, dtype=BF16), jnp.asarray(suf, dtype=BF16))


def _out_proj_body(final, hg_ref, fx_ref, x_ref, wo_ref, fnw_ref, o_ref):
    y = (_dot(hg_ref[...].astype(BF16), wo_ref[:HG_WIDTH, :])
         + _dot(fx_ref[...].astype(BF16), wo_ref[HG_WIDTH:, :]) + x_ref[...])
    if final:
        y = y * lax.rsqrt(jnp.mean(y * y, axis=-1, keepdims=True) + NORM_EPS) * fnw_ref[...]
    o_ref[...] = y


def _out_proj(hg, fx, x, wo, fnw, final, name):
    n, d_model = x.shape
    assert n % TOK_TILE == 0

    def row(width):
        return pl.BlockSpec((TOK_TILE, width), lambda i: (i, 0))

    return pl.pallas_call(
        functools.partial(_out_proj_body, final),
        grid=(n // TOK_TILE,),
        in_specs=[row(HG_WIDTH), row(FOX_WIDTH), row(d_model),
                  pl.BlockSpec(wo.shape, lambda i: (0, 0)),
                  pl.BlockSpec((1, d_model), lambda i: (0, 0))],
        out_specs=row(d_model),
        out_shape=jax.ShapeDtypeStruct((n, d_model), F32),
        compiler_params=_cparams(1),
        name=name,
    )(hg, fx, x, wo, fnw)


def _pad_rows(a, nb, dec_seq):
    a = a.reshape(nb, dec_seq, a.shape[-1])
    return jnp.pad(a, ((0, 0), (0, SAMPLE_ROWS - dec_seq), (0, 0)))


def kernel(x_prompt, x_sample, cache_k, cache_v, cache_logf, state_hgrn, page_table, norm_w, w_in,
           b_fox_f, hg_lb, hg_norm_w, w_out, final_norm_w):
    batch, seq, d_model = x_prompt.shape
    nb, dec_seq, _ = x_sample.shape
    depth = norm_w.shape[0]
    n_pool = cache_k.shape[1]
    assert cache_k.shape[2:] == (PAGE, FOX_HEADS, FOX_HEAD_DIM) and dec_seq <= SAMPLE_ROWS
    assert w_in.shape[2] == 4 * HG_WIDTH + 4 * FOX_WIDTH + FOX_HEADS

    kT = jnp.transpose(cache_k, (0, 1, 3, 4, 2)).reshape(depth, n_pool, FOX_WIDTH, PAGE)
    vT = jnp.transpose(cache_v, (0, 1, 3, 4, 2)).reshape(depth, n_pool, FOX_WIDTH, PAGE)
    lfT = jnp.transpose(cache_logf, (0, 1, 3, 2))

    tri_prompt = jnp.asarray(np.tril(np.ones((TOK_TILE, TOK_TILE), np.float32)), dtype=BF16)
    blocks = np.arange(TOK_TILE) // dec_seq
    tri_sample = jnp.asarray(np.tril(np.ones((TOK_TILE, TOK_TILE), np.float32))
                             * (blocks[:, None] == blocks[None, :]), dtype=BF16)

    xp = x_prompt.reshape(batch * seq, d_model)
    xs = x_sample.reshape(nb * dec_seq, d_model)
    n_ff = 4 * HG_WIDTH + 3 * FOX_WIDTH
    outs = {name: [] for name in ("pk", "pv", "plf", "pst", "sk", "sv", "slf", "sst")}
    for l in range(depth):
        wm = jnp.concatenate([w_in[l][:, :n_ff], w_in[l][:, n_ff + FOX_HEADS:]], axis=1).astype(BF16)
        wff = jnp.pad(w_in[l][:, n_ff:n_ff + FOX_HEADS], ((0, 0), (0, LANES - FOX_HEADS))).astype(BF16)
        bf = jnp.pad(b_fox_f[l][None, :], ((0, 0), (0, LANES - FOX_HEADS)))
        nw = norm_w[l][None, :]
        hnw = hg_norm_w[l][None, :]
        wo = w_out[l].astype(BF16)
        final = l == depth - 1
        fnw = final_norm_w[None, :]

        (hq, hk, hlf, hv, hgs, qa, ka, va, _, fk, fv, flf, fgs, _) = _in_proj(
            xp, l, nw, wm, wff, bf, hg_lb, tri_prompt, seq // TOK_TILE)
        hg, s_p = _hgrn_prompt(hq, hk, hlf, hv, hgs, hnw, batch, seq, l)
        fx = _fox_prompt(qa, ka, va, fgs, batch, seq, l)
        xp = _out_proj(hg, fx, xp, wo, fnw, final, f"out_proj_l{l}_prompt")
        outs["pk"].append(fk)
        outs["pv"].append(fv)
        outs["plf"].append(flf[:, :FOX_HEADS])
        outs["pst"].append(s_p)

        (sq, sk_, slf_, sv_, sgs, _, _, _, gq, gk, gv, glf, ggs, gcn) = _in_proj(
            xs, l, nw, wm, wff, bf, hg_lb, tri_sample, None)
        pad = functools.partial(_pad_rows, nb=nb, dec_seq=dec_seq)
        hg_s, s_s = _hgrn_sample(pad(sq), pad(sk_), pad(slf_), pad(sv_), pad(sgs), hnw, state_hgrn[l], l)
        fx_s = _fox_sample(l, page_table, kT, vT, lfT, pad(gq), pad(gk), pad(gv), pad(gcn), pad(ggs),
                           dec_seq)
        hg_s = hg_s[:, :dec_seq].reshape(nb * dec_seq, HG_WIDTH)
        fx_s = fx_s[:, :dec_seq].reshape(nb * dec_seq, FOX_WIDTH)
        xs = _out_proj(hg_s, fx_s, xs, wo, fnw, final, f"out_proj_l{l}_sample")
        outs["sk"].append(gk)
        outs["sv"].append(gv)
        outs["slf"].append(glf[:, :FOX_HEADS])
        outs["sst"].append(s_s)

    kv_p = (depth, batch, seq, FOX_HEADS, FOX_HEAD_DIM)
    kv_s = (depth, nb, dec_seq, FOX_HEADS, FOX_HEAD_DIM)
    return (xp.reshape(batch, seq, d_model),
            xs.reshape(nb, dec_seq, d_model),
            jnp.stack(outs["pk"]).reshape(kv_p),
            jnp.stack(outs["pv"]).reshape(kv_p),
            jnp.stack(outs["plf"]).reshape(depth, batch, seq, FOX_HEADS),
            jnp.stack(outs["pst"]),
            jnp.stack(outs["sk"]).reshape(kv_s),
            jnp.stack(outs["sv"]).reshape(kv_s),
            jnp.stack(outs["slf"]).reshape(depth, nb, dec_seq, FOX_HEADS),
            jnp.stack(outs["sst"]))
```
